```python
import math
import jax
import jax.numpy as jnp
from jax import lax

D_MODEL = 1024
BATCH = 2
SEQ = 8192
DEPTH = 1

D_MIX = D_MODEL
HEAD_DIM = 64
N_Q_HEADS = 8
N_KV_HEADS = 2
Q_PER_KV = N_Q_HEADS // N_KV_HEADS
ATTN_WIDTH = N_Q_HEADS * HEAD_DIM
KV_WIDTH = N_KV_HEADS * HEAD_DIM
WINDOW = 128
ATTN_BLOCK = 128
ROPE_THETA = 10000.0
S5_WIDTH = D_MIX - ATTN_WIDTH
S5_GROUP_CH = 16
S5_GROUPS = S5_WIDTH // S5_GROUP_CH
S5_STATE = 64
IN_WIDTH = ATTN_WIDTH + 2 * KV_WIDTH + S5_WIDTH
N_MEM = 256
X_HEADS = 4
X_HEAD_DIM = D_MODEL // X_HEADS
N_EXPERTS = 32
TOP_K = 4
D_EXPERT = D_MODEL
SWIGLU_LIMIT = 7.0
SWIGLU_ALPHA = 1.702
MOE_BLOCK = 512
LN_EPS = 1e-5
DEEPNORM_ALPHA = (2.0 * DEPTH) ** 0.25
DEEPNORM_BETA = (8.0 * DEPTH) ** -0.25

kernel_name = 'hybrid_swa_s5_moe_deepnorm'

F32 = jnp.float32


def layer_norm(x, g, b):
    xf = x.astype(F32)
    mu = jnp.mean(xf, -1, keepdims=True)
    var = jnp.mean(jnp.square(xf - mu), -1, keepdims=True)
    return ((xf - mu) * lax.rsqrt(var + LN_EPS) * g.astype(F32) + b.astype(F32)).astype(x.dtype)


def rope_tables(seq_len):
    half = HEAD_DIM // 2
    inv_freq = ROPE_THETA ** (-jnp.arange(half, dtype=F32) * (2.0 / HEAD_DIM))
    ang = jnp.arange(seq_len, dtype=F32)[:, None] * inv_freq[None, :]
    return jnp.cos(ang), jnp.sin(ang)


def apply_rope(x, cos, sin):
    half = HEAD_DIM // 2
    xf = x.astype(F32)
    x1, x2 = xf[..., :half], xf[..., half:]
    c = cos[None, :, None, :]
    s = sin[None, :, None, :]
    return jnp.concatenate([x1 * c - x2 * s, x2 * c + x1 * s], -1).astype(x.dtype)


def sliding_window_attention(q, k, v, sinks):
    B, L = q.shape[:2]
    nb = L // ATTN_BLOCK
    qb = q.reshape(B, nb, ATTN_BLOCK, N_KV_HEADS, Q_PER_KV, HEAD_DIM)

    def band(t):
        pad = jnp.zeros((B, ATTN_BLOCK, N_KV_HEADS, HEAD_DIM), t.dtype)
        tb = jnp.concatenate([pad, t], 1).reshape(B, nb + 1, ATTN_BLOCK, N_KV_HEADS, HEAD_DIM)
        return jnp.concatenate([tb[:, :-1], tb[:, 1:]], 2)

    kb, vb = band(k), band(v)
    s = jnp.einsum('bnqkgd,bnskd->bnkgqs', qb, kb).astype(F32) * (HEAD_DIM ** -0.5)
    qi = jnp.arange(ATTN_BLOCK)[:, None]
    si = jnp.arange(2 * ATTN_BLOCK)[None, :]
    rel = qi + ATTN_BLOCK - si
    in_window = (rel >= 0) & (rel < WINDOW)
    key_pos = (jnp.arange(nb) - 1)[:, None] * ATTN_BLOCK + si
    valid = in_window[None] & (key_pos >= 0)[:, None, :]
    s = jnp.where(valid[None, :, None, None], s, -jnp.inf)
    sink = sinks.astype(F32).reshape(N_KV_HEADS, Q_PER_KV)[None, None, :, :, None, None]
    sink = jnp.broadcast_to(sink, s.shape[:-1] + (1,))
    p = jax.nn.softmax(jnp.concatenate([s, sink], -1), axis=-1)[..., :-1]
    o = jnp.einsum('bnkgqs,bnskd->bnqkgd', p.astype(v.dtype), vb)
    return o.reshape(B, L, ATTN_WIDTH)


def s5_mixer(u, lam_re, lam_im, log_dt, b_re, b_im, c_re, c_im, d_skip, w_glu, b_glu):
    B, L, _ = u.shape
    ug = u.reshape(B, L, S5_GROUPS, S5_GROUP_CH).astype(F32)
    lr = jnp.minimum(lam_re.astype(F32), -1e-4)
    li = lam_im.astype(F32)
    dt = jnp.exp(log_dt.astype(F32))[:, None]
    mag = jnp.exp(lr * dt)
    ar = mag * jnp.cos(li * dt)
    ai = mag * jnp.sin(li * dt)
    den = lr * lr + li * li
    zr = ((ar - 1.0) * lr + ai * li) / den
    zi = (ai * lr - (ar - 1.0) * li) / den
    br = b_re.astype(F32)
    bi = b_im.astype(F32)
    bbr = zr[..., None] * br - zi[..., None] * bi
    bbi = zr[..., None] * bi + zi[..., None] * br
    bu_r = jnp.einsum('blgh,gph->blgp', ug, bbr)
    bu_i = jnp.einsum('blgh,gph->blgp', ug, bbi)
    a_r = jnp.broadcast_to(ar, bu_r.shape)
    a_i = jnp.broadcast_to(ai, bu_i.shape)

    def combine(e1, e2):
        a1r, a1i, s1r, s1i = e1
        a2r, a2i, s2r, s2i = e2
        return (a2r * a1r - a2i * a1i,
                a2r * a1i + a2i * a1r,
                a2r * s1r - a2i * s1i + s2r,
                a2r * s1i + a2i * s1r + s2i)

    _, _, st_r, st_i = lax.associative_scan(combine, (a_r, a_i, bu_r, bu_i), axis=1)
    y = (jnp.einsum('blgp,ghp->blgh', st_r, c_re.astype(F32))
         - jnp.einsum('blgp,ghp->blgh', st_i, c_im.astype(F32)))
    y = y.reshape(B, L, S5_WIDTH) + d_skip.astype(F32) * u.astype(F32)
    y = jax.nn.gelu(y).astype(u.dtype)
    return y * jax.nn.sigmoid(y @ w_glu + b_glu)


def memory_cross_attention(x, mem, w_q, w_kv, w_o):
    B, L, _ = x.shape
    q = (x @ w_q).reshape(B, L, X_HEADS, X_HEAD_DIM)
    k, v = jnp.split(mem @ w_kv, 2, axis=-1)
    k = k.reshape(B, N_MEM, X_HEADS, X_HEAD_DIM)
    v = v.reshape(B, N_MEM, X_HEADS, X_HEAD_DIM)
    s = jnp.einsum('blhd,bmhd->bhlm', q, k).astype(F32) * (X_HEAD_DIM ** -0.5)
    p = jax.nn.softmax(s, axis=-1).astype(v.dtype)
    o = jnp.einsum('bhlm,bmhd->blhd', p, v).reshape(B, L, D_MODEL)
    return o @ w_o


def clamped_swiglu(h):
    g = jnp.minimum(h[..., ::2], SWIGLU_LIMIT)
    lin = jnp.clip(h[..., 1::2], -SWIGLU_LIMIT, SWIGLU_LIMIT)
    return g * jax.nn.sigmoid(SWIGLU_ALPHA * g) * (lin + 1.0)


def moe_ffn(x2d, w_router, b_router, w1, b1, w2, b2):
    N, D = x2d.shape
    logits = (x2d @ w_router + b_router).astype(F32)
    top_val, top_idx = lax.top_k(logits, TOP_K)
    gate = jax.nn.softmax(top_val, axis=-1).astype(x2d.dtype)
    M = N * TOP_K
    flat_e = top_idx.reshape(M)
    flat_tok = jnp.arange(M, dtype=jnp.int32) // TOP_K
    order = jnp.argsort(flat_e)
    sorted_e = flat_e[order]
    sizes = jnp.bincount(flat_e, length=N_EXPERTS)
    starts = jnp.cumsum(sizes) - sizes
    padded = (sizes + MOE_BLOCK - 1) // MOE_BLOCK * MOE_BLOCK
    pends = jnp.cumsum(padded)
    pstarts = pends - padded
    dest = pstarts[sorted_e] + (jnp.arange(M) - starts[sorted_e])
    n_blocks = -(-M // MOE_BLOCK) + N_EXPERTS
    buf = jnp.zeros((n_blocks * MOE_BLOCK, D), x2d.dtype).at[dest].set(x2d[flat_tok[order]])
    block_e = jnp.clip(jnp.searchsorted(pends, jnp.arange(n_blocks) * MOE_BLOCK, side='right'),
                       0, N_EXPERTS - 1)

    def expert_block(args):
        xb, e = args
        h = xb @ w1[e] + b1[e]
        return clamped_swiglu(h) @ w2[e] + b2[e]

    out_buf = lax.map(expert_block, (buf.reshape(n_blocks, MOE_BLOCK, D), block_e))
    y_sorted = out_buf.reshape(n_blocks * MOE_BLOCK, D)[dest]
    y_assign = jnp.zeros((M, D), x2d.dtype).at[order].set(y_sorted)
    return jnp.einsum('nkd,nk->nd', y_assign.reshape(N, TOP_K, D), gate)


def setup_inputs(seed: int = 0) -> dict:
    keys = iter(jax.random.split(jax.random.key(seed), 48))

    def nrm(shape, scale):
        return scale * jax.random.normal(next(keys), shape, F32)

    Ld = DEPTH
    n_idx = jnp.arange(S5_STATE, dtype=F32)
    return {
        'x': nrm((BATCH, SEQ, D_MODEL), 1.0),
        'mem': nrm((BATCH, N_MEM, D_MODEL), 1.0),
        'w_in': nrm((Ld, D_MODEL, IN_WIDTH), D_MODEL ** -0.5),
        'b_in': nrm((Ld, IN_WIDTH), 0.02),
        'attn_sinks': nrm((Ld, N_Q_HEADS), 0.5),
        's5_lambda_re': -0.5 + nrm((Ld, S5_GROUPS, S5_STATE), 0.01),
        's5_lambda_im': math.pi * n_idx + nrm((Ld, S5_GROUPS, S5_STATE), 0.01),
        's5_log_dt': jax.random.uniform(next(keys), (Ld, S5_GROUPS), F32,
                                        math.log(0.001), math.log(0.1)),
        's5_b_re': nrm((Ld, S5_GROUPS, S5_STATE, S5_GROUP_CH), (2.0 * S5_GROUP_CH) ** -0.5),
        's5_b_im': nrm((Ld, S5_GROUPS, S5_STATE, S5_GROUP_CH), (2.0 * S5_GROUP_CH) ** -0.5),
        's5_c_re': nrm((Ld, S5_GROUPS, S5_GROUP_CH, S5_STATE), S5_STATE ** -0.5),
        's5_c_im': nrm((Ld, S5_GROUPS, S5_GROUP_CH, S5_STATE), S5_STATE ** -0.5),
        's5_d': nrm((Ld, S5_WIDTH), 1.0),
        's5_w_glu': nrm((Ld, S5_WIDTH, S5_WIDTH), S5_WIDTH ** -0.5),
        's5_b_glu': nrm((Ld, S5_WIDTH), 0.02),
        'w_out': nrm((Ld, D_MIX, D_MODEL), D_MIX ** -0.5 * DEEPNORM_BETA),
        'b_out': nrm((Ld, D_MODEL), 0.02),
        'ln1_g': 1.0 + nrm((Ld, D_MODEL), 0.02),
        'ln1_b': nrm((Ld, D_MODEL), 0.02),
        'w_xq': nrm((Ld, D_MODEL, D_MODEL), D_MODEL ** -0.5),
        'w_xkv': nrm((Ld, D_MODEL, 2 * D_MODEL), D_MODEL ** -0.5),
        'w_xo': nrm((Ld, D_MODEL, D_MODEL), D_MODEL ** -0.5 * DEEPNORM_BETA),
        'ln2_g': 1.0 + nrm((Ld, D_MODEL), 0.02),
        'ln2_b': nrm((Ld, D_MODEL), 0.02),
        'w_router': nrm((Ld, D_MODEL, N_EXPERTS), D_MODEL ** -0.5),
        'b_router': nrm((Ld, N_EXPERTS), 0.01),
        'w_e1': nrm((Ld, N_EXPERTS, D_MODEL, 2 * D_EXPERT), D_MODEL ** -0.5),
        'b_e1': nrm((Ld, N_EXPERTS, 2 * D_EXPERT), 0.02),
        'w_e2': nrm((Ld, N_EXPERTS, D_EXPERT, D_MODEL), D_EXPERT ** -0.5 * DEEPNORM_BETA),
        'b_e2': nrm((Ld, N_EXPERTS, D_MODEL), 0.02),
        'ln3_g': 1.0 + nrm((Ld, D_MODEL), 0.02),
        'ln3_b': nrm((Ld, D_MODEL), 0.02),
    }


def reference(x, mem, w_in, b_in, attn_sinks, s5_lambda_re, s5_lambda_im, s5_log_dt,
              s5_b_re, s5_b_im, s5_c_re, s5_c_im, s5_d, s5_w_glu, s5_b_glu, w_out, b_out,
              ln1_g, ln1_b, w_xq, w_xkv, w_xo, ln2_g, ln2_b, w_router, b_router,
              w_e1, b_e1, w_e2, b_e2, ln3_g, ln3_b):
    B, L, D = x.shape
    cos, sin = rope_tables(L)
    for l in range(DEPTH):
        h = x @ w_in[l] + b_in[l]
        q, k, v, u = jnp.split(h, [ATTN_WIDTH, ATTN_WIDTH + KV_WIDTH, ATTN_WIDTH + 2 * KV_WIDTH], axis=-1)
        q = apply_rope(q.reshape(B, L, N_Q_HEADS, HEAD_DIM), cos, sin)
        k = apply_rope(k.reshape(B, L, N_KV_HEADS, HEAD_DIM), cos, sin)
        v = v.reshape(B, L, N_KV_HEADS, HEAD_DIM)
        attn = sliding_window_attention(q, k, v, attn_sinks[l])
        ssm = s5_mixer(u, s5_lambda_re[l], s5_lambda_im[l], s5_log_dt[l], s5_b_re[l], s5_b_im[l],
                       s5_c_re[l], s5_c_im[l], s5_d[l], s5_w_glu[l], s5_b_glu[l])
        mix = jnp.concatenate([attn, ssm], axis=-1) @ w_out[l] + b_out[l]
        x = layer_norm(DEEPNORM_ALPHA * x + mix, ln1_g[l], ln1_b[l])
        xa = memory_cross_attention(x, mem, w_xq[l], w_xkv[l], w_xo[l])
        x = layer_norm(DEEPNORM_ALPHA * x + xa, ln2_g[l], ln2_b[l])
        ff = moe_ffn(x.reshape(B * L, D), w_router[l], b_router[l], w_e1[l], b_e1[l],
                     w_e2[l], b_e2[l]).reshape(B, L, D)
        x = layer_norm(DEEPNORM_ALPHA * x + ff, ln3_g[l], ln3_b[l])
    return x
```

```python
import functools
import math

import jax
import jax.numpy as jnp
from jax import lax
from jax.experimental import pallas as pl
from jax.experimental.pallas import tpu as pltpu

F32 = jnp.float32
BF16 = jnp.bfloat16
I32 = jnp.int32

HEAD_DIM = 64
N_Q_HEADS = 8
N_KV_HEADS = 2
Q_PER_KV = N_Q_HEADS // N_KV_HEADS
ATTN_WIDTH = N_Q_HEADS * HEAD_DIM
KV_WIDTH = N_KV_HEADS * HEAD_DIM
ATTN_BLOCK = 128
ROPE_THETA = 10000.0
S5_WIDTH = 512
S5_GROUP_CH = 16
S5_GROUPS = S5_WIDTH // S5_GROUP_CH
S5_STATE = 64
S5_QUARTERS = 4
X_HEADS = 4
N_EXPERTS = 32
TOP_K = 4
SWIGLU_LIMIT = 7.0
SWIGLU_ALPHA = 1.702
LN_EPS = 1e-5
DEPTH = 1
DEEPNORM_ALPHA = (2.0 * DEPTH) ** 0.25

LANES = 128
SUBLANES = 8
MOE_BLOCK = 256
VMEM_LIMIT = 48 * 1024 * 1024


def _params(sem, vmem=VMEM_LIMIT):
    return pltpu.CompilerParams(dimension_semantics=sem, vmem_limit_bytes=vmem)


def _layer_norm(v, g, b):
    mu = jnp.mean(v, axis=-1, keepdims=True)
    c = v - mu
    var = jnp.mean(c * c, axis=-1, keepdims=True)
    return c * lax.rsqrt(var + LN_EPS) * g + b


def _inproj_kernel(x_ref, w_ref, b_ref, cos_ref, sin_ref, q_ref, k_ref, v_ref, u_ref):
    tm = x_ref.shape[0]
    h = jnp.dot(x_ref[...].astype(BF16), w_ref[...], preferred_element_type=F32) + b_ref[...]
    cos = cos_ref[...]
    sin = sin_ref[...]
    lane = lax.broadcasted_iota(I32, (tm, LANES), 1)
    first = (lane % HEAD_DIM) < (HEAD_DIM // 2)

    def rope(c):
        rot = jnp.where(first, pltpu.roll(c, LANES - HEAD_DIM // 2, 1), pltpu.roll(c, HEAD_DIM // 2, 1))
        return c * cos + rot * sin

    scale = HEAD_DIM ** -0.5
    for j in range(ATTN_WIDTH // LANES):
        q_ref[:, j * LANES:(j + 1) * LANES] = (rope(h[:, j * LANES:(j + 1) * LANES]) * scale).astype(BF16)
    k_ref[...] = rope(h[:, ATTN_WIDTH:ATTN_WIDTH + KV_WIDTH]).astype(BF16)
    v_ref[...] = h[:, ATTN_WIDTH + KV_WIDTH:ATTN_WIDTH + 2 * KV_WIDTH].astype(BF16)
    u_ref[...] = h[:, ATTN_WIDTH + 2 * KV_WIDTH:]


def _inproj(x2d, w_bf, b, cos_t, sin_t, seq, tm=512):
    n, d = x2d.shape
    width = w_bf.shape[1]
    nt_seq = seq // tm
    full = lambda i: (0, 0)
    return pl.pallas_call(
        _inproj_kernel,
        grid=(n // tm,),
        in_specs=[
            pl.BlockSpec((tm, d), lambda i: (i, 0)),
            pl.BlockSpec((d, width), full),
            pl.BlockSpec((1, width), full),
            pl.BlockSpec((tm, LANES), lambda i: (i % nt_seq, 0)),
            pl.BlockSpec((tm, LANES), lambda i: (i % nt_seq, 0)),
        ],
        out_specs=[
            pl.BlockSpec((tm, ATTN_WIDTH), lambda i: (i, 0)),
            pl.BlockSpec((tm, KV_WIDTH), lambda i: (i, 0)),
            pl.BlockSpec((tm, KV_WIDTH), lambda i: (i, 0)),
            pl.BlockSpec((tm, S5_WIDTH), lambda i: (i, 0)),
        ],
        out_shape=[
            jax.ShapeDtypeStruct((n, ATTN_WIDTH), BF16),
            jax.ShapeDtypeStruct((n, KV_WIDTH), BF16),
            jax.ShapeDtypeStruct((n, KV_WIDTH), BF16),
            jax.ShapeDtypeStruct((n, S5_WIDTH), F32),
        ],
        compiler_params=_params(("arbitrary",)),
        name="inproj_rope",
    )(x2d, w_bf, b, cos_t, sin_t)


def _swa_kernel(sink_ref, q_ref, kp_ref, kc_ref, vp_ref, vc_ref, o_ref):
    j = pl.program_id(1)
    blk = q_ref.shape[1]
    q = q_ref[0]
    kk = jnp.concatenate([kp_ref[0], kc_ref[0]], axis=0)
    vv = jnp.concatenate([vp_ref[0], vc_ref[0]], axis=0)
    qi = lax.broadcasted_iota(I32, (blk, 2 * blk), 0)
    si = lax.broadcasted_iota(I32, (blk, 2 * blk), 1)
    rel = qi + blk - si
    valid = (rel >= 0) & (rel < blk) & ((si >= blk) | (j > 0))
    for g in range(N_KV_HEADS):
        kg = kk[:, g * HEAD_DIM:(g + 1) * HEAD_DIM]
        vg = vv[:, g * HEAD_DIM:(g + 1) * HEAD_DIM]
        for hh in range(Q_PER_KV):
            head = g * Q_PER_KV + hh
            qh = q[:, head * HEAD_DIM:(head + 1) * HEAD_DIM]
            s = lax.dot_general(qh, kg, (((1,), (1,)), ((), ())), preferred_element_type=F32)
            s = jnp.where(valid, s, -1e30)
            sink = sink_ref[head]
            m = jnp.maximum(jnp.max(s, axis=1, keepdims=True), sink)
            p = jnp.exp(s - m)
            denom = jnp.sum(p, axis=1, keepdims=True) + jnp.exp(sink - m)
            o = jnp.dot(p.astype(BF16), vg, preferred_element_type=F32) / denom
            o_ref[0, :, head * HEAD_DIM:(head + 1) * HEAD_DIM] = o.astype(BF16)


def _swa(sinks, q, k, v):
    b, seq, _ = q.shape
    nb = seq // ATTN_BLOCK
    prev = lambda bi, j: (bi, jnp.maximum(j - 1, 0), 0)
    cur = lambda bi, j: (bi, j, 0)
    return pl.pallas_call(
        _swa_kernel,
        grid=(b, nb),
        in_specs=[
            pl.BlockSpec(memory_space=pltpu.SMEM),
            pl.BlockSpec((1, ATTN_BLOCK, ATTN_WIDTH), cur),
            pl.BlockSpec((1, ATTN_BLOCK, KV_WIDTH), prev),
            pl.BlockSpec((1, ATTN_BLOCK, KV_WIDTH), cur),
            pl.BlockSpec((1, ATTN_BLOCK, KV_WIDTH), prev),
            pl.BlockSpec((1, ATTN_BLOCK, KV_WIDTH), cur),
        ],
        out_specs=pl.BlockSpec((1, ATTN_BLOCK, ATTN_WIDTH), cur),
        out_shape=jax.ShapeDtypeStruct((b, seq, ATTN_WIDTH), BF16),
        compiler_params=_params(("arbitrary", "arbitrary")),
        name="swa",
    )(sinks, q, k, k, v, v)


def _s5_prep_kernel(lr_ref, li_ref, ldt_ref, br_ref, bi_ref, ar_ref, ai_ref, bbr_ref, bbi_ref):
    lr = jnp.minimum(lr_ref[...], -1e-4)
    li = li_ref[...]
    dt = jnp.exp(ldt_ref[...])
    mag = jnp.exp(lr * dt)
    ar = mag * jnp.cos(li * dt)
    ai = mag * jnp.sin(li * dt)
    den = lr * lr + li * li
    zr = ((ar - 1.0) * lr + ai * li) / den
    zi = (ai * lr - (ar - 1.0) * li) / den
    ar_ref[...] = ar
    ai_ref[...] = ai
    br = br_ref[...]
    bi = bi_ref[...]
    bbr_ref[...] = zr[:, None, :] * br - zi[:, None, :] * bi
    bbi_ref[...] = zr[:, None, :] * bi + zi[:, None, :] * br


def _s5_prep(lam_re, lam_im, log_dt, b_re_t, b_im_t):
    g, p = lam_re.shape
    h = b_re_t.shape[1]
    return pl.pallas_call(
        _s5_prep_kernel,
        out_shape=[
            jax.ShapeDtypeStruct((g, p), F32),
            jax.ShapeDtypeStruct((g, p), F32),
            jax.ShapeDtypeStruct((g, h, p), F32),
            jax.ShapeDtypeStruct((g, h, p), F32),
        ],
        name="s5_prep",
    )(lam_re, lam_im, log_dt.reshape(g, 1), b_re_t, b_im_t)


def _s5_kernel(u_ref, wb_ref, ar_ref, ai_ref, wc_ref, d_ref, wg_ref, bg_ref, o_ref,
               st_ref, lhs_ref, buf_ref, y_ref):
    nb, t_len, width = u_ref.shape
    half = buf_ref.shape[1] // 2
    qw = width // S5_QUARTERS

    @pl.when(pl.program_id(0) == 0)
    def _():
        st_ref[...] = jnp.zeros_like(st_ref)
        lhs_ref[...] = jnp.zeros_like(lhs_ref)

    for bi in range(nb):
        for j in range(S5_QUARTERS):
            lhs_ref[j, pl.ds(bi * S5_QUARTERS + j, t_len, stride=SUBLANES), :] = u_ref[bi, :, j * qw:(j + 1) * qw]
    lhs = jnp.concatenate([lhs_ref[c] for c in range(S5_QUARTERS)], axis=1).astype(BF16)
    buf_ref[...] = jnp.dot(lhs, wb_ref[...], preferred_element_type=F32)

    ar = ar_ref[...]
    ai = ai_ref[...]

    def step(t, carry):
        sr, si = carry
        r0 = pl.multiple_of(t * SUBLANES, SUBLANES)
        nsr = ar * sr - ai * si + buf_ref[pl.ds(r0, SUBLANES), 0:half]
        nsi = ar * si + ai * sr + buf_ref[pl.ds(r0, SUBLANES), half:2 * half]
        buf_ref[pl.ds(r0, SUBLANES), 0:half] = nsr
        buf_ref[pl.ds(r0, SUBLANES), half:2 * half] = nsi
        return nsr, nsi

    sr, si = lax.fori_loop(0, t_len, step, (st_ref[:, 0:half], st_ref[:, half:2 * half]), unroll=8)
    st_ref[:, 0:half] = sr
    st_ref[:, half:2 * half] = si

    yall = jnp.dot(buf_ref[...].astype(BF16), wc_ref[...], preferred_element_type=F32)
    for c in range(S5_QUARTERS):
        y_ref[c] = yall[:, c * qw:(c + 1) * qw]
    for bi in range(nb):
        y = jnp.concatenate([y_ref[j, pl.ds(bi * S5_QUARTERS + j, t_len, stride=SUBLANES), :]
                             for j in range(S5_QUARTERS)], axis=1)
        g = jax.nn.gelu(y + d_ref[...] * u_ref[bi])
        z = jnp.dot(g.astype(BF16), wg_ref[...], preferred_element_type=F32) + bg_ref[...]
        o_ref[bi] = (g * jax.nn.sigmoid(z)).astype(BF16)


def _s5(u, wb, ar8, ai8, wc, d_skip, w_glu_bf, b_glu, t_len=256):
    b, seq, width = u.shape
    rows = b * S5_QUARTERS
    assert rows == SUBLANES and width // S5_QUARTERS == LANES
    sw = wb.shape[1]
    full = lambda i: (0, 0)
    return pl.pallas_call(
        _s5_kernel,
        grid=(seq // t_len,),
        in_specs=[
            pl.BlockSpec((b, t_len, width), lambda i: (0, i, 0)),
            pl.BlockSpec(wb.shape, full),
            pl.BlockSpec(ar8.shape, full),
            pl.BlockSpec(ai8.shape, full),
            pl.BlockSpec(wc.shape, full),
            pl.BlockSpec((1, width), full),
            pl.BlockSpec((width, width), full),
            pl.BlockSpec((1, width), full),
        ],
        out_specs=pl.BlockSpec((b, t_len, width), lambda i: (0, i, 0)),
        out_shape=jax.ShapeDtypeStruct((b, seq, width), BF16),
        scratch_shapes=[
            pltpu.VMEM((rows, sw), F32),
            pltpu.VMEM((S5_QUARTERS, rows * t_len, width // S5_QUARTERS), F32),
            pltpu.VMEM((rows * t_len, sw), F32),
            pltpu.VMEM((S5_QUARTERS, rows * t_len, width // S5_QUARTERS), F32),
        ],
        compiler_params=_params(("arbitrary",)),
        name="s5_scan",
    )(u, wb, ar8, ai8, wc, d_skip, w_glu_bf, b_glu)


def _outproj_kernel(a_ref, s_ref, x_ref, wa_ref, ws_ref, b_ref, g_ref, be_ref, o_ref):
    mix = (jnp.dot(a_ref[...], wa_ref[...], preferred_element_type=F32)
           + jnp.dot(s_ref[...], ws_ref[...], preferred_element_type=F32) + b_ref[...])
    o_ref[...] = _layer_norm(DEEPNORM_ALPHA * x_ref[...] + mix, g_ref[...], be_ref[...])


def _outproj(attn, ssm, x2d, wa, ws, b, g, be, tm=512):
    n, d = x2d.shape
    full = lambda i: (0, 0)
    row = lambda i: (i, 0)
    return pl.pallas_call(
        _outproj_kernel,
        grid=(n // tm,),
        in_specs=[
            pl.BlockSpec((tm, attn.shape[1]), row),
            pl.BlockSpec((tm, ssm.shape[1]), row),
            pl.BlockSpec((tm, d), row),
            pl.BlockSpec(wa.shape, full),
            pl.BlockSpec(ws.shape, full),
            pl.BlockSpec((1, d), full),
            pl.BlockSpec((1, d), full),
            pl.BlockSpec((1, d), full),
        ],
        out_specs=pl.BlockSpec((tm, d), row),
        out_shape=jax.ShapeDtypeStruct((n, d), F32),
        compiler_params=_params(("arbitrary",)),
        name="outproj_ln1",
    )(attn, ssm, x2d, wa, ws, b, g, be)


def _memkv_kernel(m_ref, w_ref, k_ref, v_ref):
    d = k_ref.shape[1]
    kv = jnp.dot(m_ref[...].astype(BF16), w_ref[...], preferred_element_type=F32)
    k_ref[...] = kv[:, :d].astype(BF16)
    v_ref[...] = kv[:, d:].astype(BF16)


def _memkv(mem2d, w_bf):
    n, d = mem2d.shape
    return pl.pallas_call(
        _memkv_kernel,
        out_shape=[jax.ShapeDtypeStruct((n, d), BF16), jax.ShapeDtypeStruct((n, d), BF16)],
        compiler_params=_params(None),
        name="mem_kv",
    )(mem2d, w_bf)


def _xattn_kernel(x_ref, wq_ref, k_ref, v_ref, wo_ref, g_ref, be_ref, wr_ref, br_ref,
                  x2_ref, idx_ref, gate_ref):
    tm, d = x_ref.shape
    hd = d // X_HEADS
    x1 = x_ref[...]
    q = jnp.dot(x1.astype(BF16), wq_ref[...], preferred_element_type=F32) * (hd ** -0.5)
    q = q.astype(BF16)
    k = k_ref[0]
    v = v_ref[0]
    outs = []
    for h in range(X_HEADS):
        sl = slice(h * hd, (h + 1) * hd)
        s = lax.dot_general(q[:, sl], k[:, sl], (((1,), (1,)), ((), ())), preferred_element_type=F32)
        m = jnp.max(s, axis=1, keepdims=True)
        p = jnp.exp(s - m)
        denom = jnp.sum(p, axis=1, keepdims=True)
        outs.append((jnp.dot(p.astype(BF16), v[:, sl], preferred_element_type=F32) / denom).astype(BF16))
    o = jnp.concatenate(outs, axis=1)
    xa = jnp.dot(o, wo_ref[...], preferred_element_type=F32)
    x2 = _layer_norm(DEEPNORM_ALPHA * x1 + xa, g_ref[...], be_ref[...])
    x2_ref[...] = x2

    logits = jnp.dot(x2, wr_ref[...], preferred_element_type=F32, precision=lax.Precision.HIGHEST) + br_ref[...]
    ne = logits.shape[1]
    lane = lax.broadcasted_iota(I32, (tm, ne), 1).astype(F32)
    lane_out = lax.broadcasted_iota(I32, (tm, TOP_K), 1)
    vals, idxs = [], []
    l = logits
    for _ in range(TOP_K):
        m = jnp.max(l, axis=1, keepdims=True)
        ix = jnp.min(jnp.where(l == m, lane, float(ne)), axis=1, keepdims=True)
        vals.append(m)
        idxs.append(ix)
        l = jnp.where(lane == ix, -jnp.inf, l)
    es = [jnp.exp(vk - vals[0]) for vk in vals]
    tot = es[0] + es[1] + es[2] + es[3]
    idx_out = jnp.zeros((tm, TOP_K), F32)
    gate_out = jnp.zeros((tm, TOP_K), F32)
    for kk in range(TOP_K):
        idx_out = jnp.where(lane_out == kk, idxs[kk], idx_out)
        gate_out = jnp.where(lane_out == kk, es[kk] / tot, gate_out)
    idx_ref[...] = idx_out.astype(I32)
    gate_ref[...] = gate_out


def _xattn(x1, wq, kmem, vmem, wo, g, be, w_router, b_router, seq, tm=256):
    n, d = x1.shape
    nm = kmem.shape[1]
    ne = w_router.shape[1]
    full = lambda i: (0, 0)
    row = lambda i: (i, 0)
    mem_map = lambda i: ((i * tm) // seq, 0, 0)
    return pl.pallas_call(
        _xattn_kernel,
        grid=(n // tm,),
        in_specs=[
            pl.BlockSpec((tm, d), row),
            pl.BlockSpec((d, d), full),
            pl.BlockSpec((1, nm, d), mem_map),
            pl.BlockSpec((1, nm, d), mem_map),
            pl.BlockSpec((d, d), full),
            pl.BlockSpec((1, d), full),
            pl.BlockSpec((1, d), full),
            pl.BlockSpec((d, ne), full),
            pl.BlockSpec((1, ne), full),
        ],
        out_specs=[
            pl.BlockSpec((tm, d), row),
            pl.BlockSpec((tm, TOP_K), row),
            pl.BlockSpec((tm, TOP_K), row),
        ],
        out_shape=[
            jax.ShapeDtypeStruct((n, d), F32),
            jax.ShapeDtypeStruct((n, TOP_K), I32),
            jax.ShapeDtypeStruct((n, TOP_K), F32),
        ],
        compiler_params=_params(("arbitrary",)),
        name="xattn_ln2_router",
    )(x1, wq, kmem, vmem, wo, g, be, w_router, b_router)


def _route_kernel(idx_ref, dest_ref, meta_ref, cnt_ref, carry_ref, pstart_ref):
    ph = pl.program_id(0)
    i = pl.program_id(1)
    tm = idx_ref.shape[0]
    ne = carry_ref.shape[1]
    idx = idx_ref[...]
    lane = lax.broadcasted_iota(I32, (tm, ne), 1)
    r0 = pl.multiple_of(i * tm, tm)

    @pl.when((ph == 0) & (i == 0))
    def _():
        carry_ref[...] = jnp.zeros_like(carry_ref)

    @pl.when(ph == 0)
    def _():
        oh = jnp.zeros((tm, ne), F32)
        for k in range(TOP_K):
            oh = oh + (idx[:, k:k + 1] == lane).astype(F32)
        rr = lax.broadcasted_iota(I32, (tm, tm), 0)
        cc = lax.broadcasted_iota(I32, (tm, tm), 1)
        tri = (rr > cc).astype(BF16)
        pre = jnp.dot(tri, oh.astype(BF16), preferred_element_type=F32) + carry_ref[...]
        cnt_ref[pl.ds(r0, tm), :] = pre
        carry_ref[...] = carry_ref[...] + jnp.sum(oh, axis=0, keepdims=True)

    @pl.when((ph == 1) & (i == 0))
    def _():
        tot = jnp.broadcast_to(carry_ref[...], (SUBLANES, ne))
        padded = jnp.floor((tot + (MOE_BLOCK - 1)) * (1.0 / MOE_BLOCK)) * MOE_BLOCK
        er = lax.broadcasted_iota(I32, (ne, ne), 0)
        ec = lax.broadcasted_iota(I32, (ne, ne), 1)
        upper = (er < ec).astype(F32)
        ps = jnp.dot(padded, upper, preferred_element_type=F32, precision=lax.Precision.HIGHEST)
        pstart_ref[...] = ps[0:1, :]
        row = lax.broadcasted_iota(I32, (SUBLANES, ne), 0)
        meta_ref[...] = jnp.where(row == 0, tot, jnp.where(row == 1, ps, 0.0)).astype(I32)

    @pl.when(ph == 1)
    def _():
        pos = cnt_ref[pl.ds(r0, tm), :] + pstart_ref[...]
        lane_out = lax.broadcasted_iota(I32, (tm, TOP_K), 1)
        out = jnp.zeros((tm, TOP_K), F32)
        for k in range(TOP_K):
            dk = jnp.sum(jnp.where(idx[:, k:k + 1] == lane, pos, 0.0), axis=1, keepdims=True)
            out = jnp.where(lane_out == k, dk, out)
        dest_ref[...] = out.astype(I32)


def _route(idx, tm=512):
    n = idx.shape[0]
    return pl.pallas_call(
        _route_kernel,
        grid=(2, n // tm),
        in_specs=[pl.BlockSpec((tm, TOP_K), lambda p, i: (i, 0))],
        out_specs=[
            pl.BlockSpec((tm, TOP_K), lambda p, i: (i * p, 0)),
            pl.BlockSpec((SUBLANES, N_EXPERTS), lambda p, i: (0, 0)),
        ],
        out_shape=[
            jax.ShapeDtypeStruct((n, TOP_K), I32),
            jax.ShapeDtypeStruct((SUBLANES, N_EXPERTS), I32),
        ],
        scratch_shapes=[
            pltpu.VMEM((n, N_EXPERTS), F32),
            pltpu.VMEM((1, N_EXPERTS), F32),
            pltpu.VMEM((1, N_EXPERTS), F32),
        ],
        compiler_params=_params(("arbitrary", "arbitrary")),
        name="route",
    )(idx)


def _dispatch_kernel(dest_ref, x_ref, zero_ref, buf_ref, sem):
    del zero_ref
    i = pl.program_id(0)
    tm = x_ref.shape[0]
    base = i * (tm * TOP_K)

    def row_copy(r, d):
        return pltpu.make_async_copy(x_ref.at[pl.ds(r, 1)], buf_ref.at[pl.ds(d, 1)], sem)

    def issue(r, c):
        for k in range(TOP_K):
            row_copy(r, dest_ref[base + r * TOP_K + k]).start()
        return c

    lax.fori_loop(0, tm, issue, 0)

    def drain(r, c):
        for k in range(TOP_K):
            row_copy(r, 0).wait()
        return c

    lax.fori_loop(0, tm, drain, 0)


def _dispatch(dest_flat, x2, n_rows, tm=256):
    n, d = x2.shape
    zeros = jnp.zeros((n_rows, d), F32)
    return pl.pallas_call(
        _dispatch_kernel,
        grid_spec=pltpu.PrefetchScalarGridSpec(
            num_scalar_prefetch=1,
            grid=(n // tm,),
            in_specs=[
                pl.BlockSpec((tm, d), lambda i, dref: (i, 0)),
                pl.BlockSpec(memory_space=pl.ANY),
            ],
            out_specs=pl.BlockSpec(memory_space=pl.ANY),
            scratch_shapes=[pltpu.SemaphoreType.DMA],
        ),
        out_shape=jax.ShapeDtypeStruct((n_rows, d), F32),
        input_output_aliases={2: 0},
        compiler_params=_params(("arbitrary",)),
        name="dispatch",
    )(dest_flat, x2, zeros)


def _expert_kernel(be_ref, nu_ref, x_ref, w1_ref, w2_ref, b1g_ref, b1l_ref, b2_ref, perm_ref, o_ref,
                   wg_ref, wl_ref, w2b_ref):
    i = pl.program_id(0)
    e = be_ref[i]
    prev = be_ref[jnp.maximum(i - 1, 0)]
    used = i < nu_ref[0]
    half = perm_ref.shape[0] // 2

    @pl.when(used & ((i == 0) | (e != prev)))
    def _():
        perm = perm_ref[...]
        for c in range(w1_ref.shape[2] // (2 * half)):
            w1c = w1_ref[0, :, c * 2 * half:(c + 1) * 2 * half].astype(BF16)
            sep = jnp.dot(w1c, perm, preferred_element_type=F32).astype(BF16)
            wg_ref[:, c * half:(c + 1) * half] = sep[:, :half]
            wl_ref[:, c * half:(c + 1) * half] = sep[:, half:]
        w2b_ref[...] = w2_ref[0].astype(BF16)

    @pl.when(used)
    def _():
        xb = x_ref[...].astype(BF16)
        hg = jnp.dot(xb, wg_ref[...], preferred_element_type=F32) + b1g_ref[0]
        hl = jnp.dot(xb, wl_ref[...], preferred_element_type=F32) + b1l_ref[0]
        gt = jnp.minimum(hg, SWIGLU_LIMIT)
        lin = jnp.clip(hl, -SWIGLU_LIMIT, SWIGLU_LIMIT)
        act = gt * jax.nn.sigmoid(SWIGLU_ALPHA * gt) * (lin + 1.0)
        o_ref[...] = jnp.dot(act.astype(BF16), w2b_ref[...], preferred_element_type=F32) + b2_ref[0]

    @pl.when(jnp.logical_not(used))
    def _():
        o_ref[...] = jnp.zeros_like(o_ref)


def _experts(block_e, n_used, buf, w1, w2, b1g, b1l, b2, perm):
    n_rows, d = buf.shape
    nblk = n_rows // MOE_BLOCK
    ne, _, d2 = w1.shape
    de = w2.shape[1]
    wmap = lambda i, be, nu: (be[i], 0, 0)
    return pl.pallas_call(
        _expert_kernel,
        grid_spec=pltpu.PrefetchScalarGridSpec(
            num_scalar_prefetch=2,
            grid=(nblk,),
            in_specs=[
                pl.BlockSpec((MOE_BLOCK, d), lambda i, be, nu: (i, 0)),
                pl.BlockSpec((1, d, d2), wmap),
                pl.BlockSpec((1, de, d), wmap),
                pl.BlockSpec((1, 1, de), wmap),
                pl.BlockSpec((1, 1, de), wmap),
                pl.BlockSpec((1, 1, d), wmap),
                pl.BlockSpec(perm.shape, lambda i, be, nu: (0, 0)),
            ],
            out_specs=pl.BlockSpec((MOE_BLOCK, d), lambda i, be, nu: (i, 0)),
            scratch_shapes=[
                pltpu.VMEM((d, de), BF16),
                pltpu.VMEM((d, de), BF16),
                pltpu.VMEM((de, d), BF16),
            ],
        ),
        out_shape=jax.ShapeDtypeStruct((n_rows, d), F32),
        compiler_params=_params(("arbitrary",)),
        name="experts",
    )(block_e, n_used, buf, w1, w2, b1g, b1l, b2, perm)


def _combine_kernel(dest_ref, y_ref, x_ref, gate_ref, g_ref, be_ref, o_ref, rows_ref, sem):
    i = pl.program_id(0)
    tm = x_ref.shape[0]
    base = i * (tm * TOP_K)

    def row_copy(r, k, d):
        return pltpu.make_async_copy(y_ref.at[pl.ds(d, 1)], rows_ref.at[k, pl.ds(r, 1)], sem)

    def issue(r, c):
        for k in range(TOP_K):
            row_copy(r, k, dest_ref[base + r * TOP_K + k]).start()
        return c

    lax.fori_loop(0, tm, issue, 0)

    def drain(r, c):
        for k in range(TOP_K):
            row_copy(r, k, 0).wait()
        return c

    lax.fori_loop(0, tm, drain, 0)

    gate = gate_ref[...]
    ff = gate[:, 0:1] * rows_ref[0]
    for k in range(1, TOP_K):
        ff = ff + gate[:, k:k + 1] * rows_ref[k]
    o_ref[...] = _layer_norm(DEEPNORM_ALPHA * x_ref[...] + ff, g_ref[...], be_ref[...])


def _combine(dest_flat, ybuf, x2, gate, g, be, tm=256):
    n, d = x2.shape
    full = lambda i, dref: (0, 0)
    row = lambda i, dref: (i, 0)
    return pl.pallas_call(
        _combine_kernel,
        grid_spec=pltpu.PrefetchScalarGridSpec(
            num_scalar_prefetch=1,
            grid=(n // tm,),
            in_specs=[
                pl.BlockSpec(memory_space=pl.ANY),
                pl.BlockSpec((tm, d), row),
                pl.BlockSpec((tm, TOP_K), row),
                pl.BlockSpec((1, d), full),
                pl.BlockSpec((1, d), full),
            ],
            out_specs=pl.BlockSpec((tm, d), row),
            scratch_shapes=[pltpu.VMEM((TOP_K, tm, d), F32), pltpu.SemaphoreType.DMA],
        ),
        out_shape=jax.ShapeDtypeStruct((n, d), F32),
        compiler_params=_params(("arbitrary",)),
        name="combine_ln3",
    )(dest_flat, ybuf, x2, gate, g, be)


def _rope_tables(seq):
    half = HEAD_DIM // 2
    inv_freq = ROPE_THETA ** (-jnp.arange(half, dtype=F32) * (2.0 / HEAD_DIM))
    ang = jnp.arange(seq, dtype=F32)[:, None] * inv_freq[None, :]
    cos, sin = jnp.cos(ang), jnp.sin(ang)
    reps = LANES // HEAD_DIM
    cos_t = jnp.tile(jnp.concatenate([cos, cos], axis=1), (1, reps))
    sin_t = jnp.tile(jnp.concatenate([-sin, sin], axis=1), (1, reps))
    return cos_t, sin_t


def _s5_matrices(ar, ai, bbr, bbi, c_re_t, c_im_t, batch):
    gq = S5_GROUPS // S5_QUARTERS
    eye = jnp.eye(gq, dtype=F32)

    def in_mat(bb):
        bb4 = bb.reshape(S5_QUARTERS, gq, S5_GROUP_CH, S5_STATE)
        return jnp.einsum('jghp,gk->jghkp', bb4, eye).reshape(S5_WIDTH, gq * S5_STATE)

    def out_mat(cc):
        cc4 = cc.reshape(S5_QUARTERS, gq, S5_GROUP_CH, S5_STATE)
        return jnp.einsum('jghp,gk->kpjgh', cc4, eye).reshape(gq * S5_STATE, S5_WIDTH)

    wb = jnp.concatenate([in_mat(bbr), in_mat(bbi)], axis=1).astype(BF16)
    wc = jnp.concatenate([out_mat(c_re_t), -out_mat(c_im_t)], axis=0).astype(BF16)
    ar8 = jnp.tile(ar.reshape(S5_QUARTERS, gq * S5_STATE), (batch, 1))
    ai8 = jnp.tile(ai.reshape(S5_QUARTERS, gq * S5_STATE), (batch, 1))
    return wb, wc, ar8, ai8


def _deinterleave_perm():
    n = 2 * LANES
    src = jnp.arange(n)
    dst = jnp.where(src % 2 == 0, src // 2, LANES + src // 2)
    return (dst[:, None] == jnp.arange(n)[None, :]).astype(BF16)


def kernel(x, mem, w_in, b_in, attn_sinks, s5_lambda_re, s5_lambda_im, s5_log_dt, s5_b_re, s5_b_im, s5_c_re, s5_c_im, s5_d, s5_w_glu, s5_b_glu, w_out, b_out, ln1_g, ln1_b, w_xq, w_xkv, w_xo, ln2_g, ln2_b, w_router, b_router, w_e1, b_e1, w_e2, b_e2, ln3_g, ln3_b):
    B, L, D = x.shape
    N = B * L
    cos_t, sin_t = _rope_tables(L)
    perm = _deinterleave_perm()
    row2 = lambda a: a.reshape(1, -1)
    h = x.reshape(N, D)
    for l in range(DEPTH):
        q, k, v, u = _inproj(h, w_in[l].astype(BF16), row2(b_in[l]), cos_t, sin_t, L)
        attn = _swa(attn_sinks[l], q.reshape(B, L, -1), k.reshape(B, L, -1), v.reshape(B, L, -1))
        ar, ai, bbr, bbi = _s5_prep(s5_lambda_re[l], s5_lambda_im[l], s5_log_dt[l],
                                    jnp.swapaxes(s5_b_re[l], 1, 2), jnp.swapaxes(s5_b_im[l], 1, 2))
        wb, wc, ar8, ai8 = _s5_matrices(ar, ai, bbr, bbi, s5_c_re[l], s5_c_im[l], B)
        ssm = _s5(u.reshape(B, L, -1), wb, ar8, ai8, wc, row2(s5_d[l]),
                  s5_w_glu[l].astype(BF16), row2(s5_b_glu[l]))
        wo = w_out[l].astype(BF16)
        x1 = _outproj(attn.reshape(N, -1), ssm.reshape(N, -1), h, wo[:ATTN_WIDTH], wo[ATTN_WIDTH:],
                      row2(b_out[l]), row2(ln1_g[l]), row2(ln1_b[l]))
        kmem, vmem = _memkv(mem.reshape(-1, D), w_xkv[l].astype(BF16))
        x2, idx, gate = _xattn(x1, w_xq[l].astype(BF16), kmem.reshape(B, -1, D), vmem.reshape(B, -1, D),
                               w_xo[l].astype(BF16), row2(ln2_g[l]), row2(ln2_b[l]),
                               w_router[l], row2(b_router[l]), L)
        dest, meta = _route(idx)
        padded = (meta[0] + (MOE_BLOCK - 1)) // MOE_BLOCK * MOE_BLOCK
        pends = meta[1] + padded
        n_blocks = N * TOP_K // MOE_BLOCK + N_EXPERTS
        block_e = jnp.clip(jnp.searchsorted(pends, jnp.arange(n_blocks, dtype=I32) * MOE_BLOCK, side='right'),
                           0, N_EXPERTS - 1).astype(I32)
        n_used = (pends[-1:] // MOE_BLOCK).astype(I32)
        dest_flat = dest.reshape(N * TOP_K)
        buf = _dispatch(dest_flat, x2, n_blocks * MOE_BLOCK)
        de = w_e2.shape[2]
        ybuf = _experts(block_e, n_used, buf, w_e1[l], w_e2[l],
                        b_e1[l][:, 0::2].reshape(N_EXPERTS, 1, de), b_e1[l][:, 1::2].reshape(N_EXPERTS, 1, de),
                        b_e2[l].reshape(N_EXPERTS, 1, D), perm)
        h = _combine(dest_flat, ybuf, x2, gate, row2(ln3_g[l]), row2(ln3_b[l]))
    return h.reshape(B, L, D)
```

```python
import functools
import math

import jax
import jax.numpy as jnp
from jax import lax
from jax.experimental import pallas as pl
from jax.experimental.pallas import tpu as pltpu

F32 = jnp.float32
BF16 = jnp.bfloat16
I32 = jnp.int32

HEAD_DIM = 64
N_Q_HEADS = 8
N_KV_HEADS = 2
Q_PER_KV = N_Q_HEADS // N_KV_HEADS
ATTN_WIDTH = N_Q_HEADS * HEAD_DIM
KV_WIDTH = N_KV_HEADS * HEAD_DIM
ATTN_BLOCK = 128
ROPE_THETA = 10000.0
S5_WIDTH = 512
S5_GROUP_CH = 16
S5_GROUPS = S5_WIDTH // S5_GROUP_CH
S5_STATE = 64
S5_QUARTERS = 4
X_HEADS = 4
N_EXPERTS = 32
TOP_K = 4
SWIGLU_LIMIT = 7.0
SWIGLU_ALPHA = 1.702
LN_EPS = 1e-5
DEPTH = 1
DEEPNORM_ALPHA = (2.0 * DEPTH) ** 0.25

LANES = 128
SUBLANES = 8
MOE_BLOCK = 256
INVERT_STEPS = 16
VMEM_LIMIT = 48 * 1024 * 1024


def _params(sem, vmem=VMEM_LIMIT):
    return pltpu.CompilerParams(dimension_semantics=sem, vmem_limit_bytes=vmem)


def _layer_norm(v, g, b):
    mu = jnp.mean(v, axis=-1, keepdims=True)
    c = v - mu
    var = jnp.mean(c * c, axis=-1, keepdims=True)
    return c * lax.rsqrt(var + LN_EPS) * g + b


def _inproj_kernel(x_ref, w_ref, b_ref, cos_ref, sin_ref, q_ref, k_ref, v_ref, u_ref):
    tm = x_ref.shape[0]
    h = jnp.dot(x_ref[...].astype(BF16), w_ref[...], preferred_element_type=F32) + b_ref[...]
    cos = cos_ref[...]
    sin = sin_ref[...]
    lane = lax.broadcasted_iota(I32, (tm, LANES), 1)
    first = (lane % HEAD_DIM) < (HEAD_DIM // 2)

    def rope(c):
        rot = jnp.where(first, pltpu.roll(c, LANES - HEAD_DIM // 2, 1), pltpu.roll(c, HEAD_DIM // 2, 1))
        return c * cos + rot * sin

    scale = HEAD_DIM ** -0.5
    for j in range(ATTN_WIDTH // LANES):
        q_ref[:, j * LANES:(j + 1) * LANES] = (rope(h[:, j * LANES:(j + 1) * LANES]) * scale).astype(BF16)
    k_ref[...] = rope(h[:, ATTN_WIDTH:ATTN_WIDTH + KV_WIDTH]).astype(BF16)
    v_ref[...] = h[:, ATTN_WIDTH + KV_WIDTH:ATTN_WIDTH + 2 * KV_WIDTH].astype(BF16)
    u_ref[...] = h[:, ATTN_WIDTH + 2 * KV_WIDTH:]


def _inproj(x2d, w_bf, b, cos_t, sin_t, seq, tm=512):
    n, d = x2d.shape
    width = w_bf.shape[1]
    nt_seq = seq // tm
    full = lambda i: (0, 0)
    return pl.pallas_call(
        _inproj_kernel,
        grid=(n // tm,),
        in_specs=[
            pl.BlockSpec((tm, d), lambda i: (i, 0)),
            pl.BlockSpec((d, width), full),
            pl.BlockSpec((1, width), full),
            pl.BlockSpec((tm, LANES), lambda i: (i % nt_seq, 0)),
            pl.BlockSpec((tm, LANES), lambda i: (i % nt_seq, 0)),
        ],
        out_specs=[
            pl.BlockSpec((tm, ATTN_WIDTH), lambda i: (i, 0)),
            pl.BlockSpec((tm, KV_WIDTH), lambda i: (i, 0)),
            pl.BlockSpec((tm, KV_WIDTH), lambda i: (i, 0)),
            pl.BlockSpec((tm, S5_WIDTH), lambda i: (i, 0)),
        ],
        out_shape=[
            jax.ShapeDtypeStruct((n, ATTN_WIDTH), BF16),
            jax.ShapeDtypeStruct((n, KV_WIDTH), BF16),
            jax.ShapeDtypeStruct((n, KV_WIDTH), BF16),
            jax.ShapeDtypeStruct((n, S5_WIDTH), F32),
        ],
        compiler_params=_params(("arbitrary",)),
        name="inproj_rope",
    )(x2d, w_bf, b, cos_t, sin_t)


def _swa_kernel(sink_ref, q_ref, kp_ref, kc_ref, vp_ref, vc_ref, o_ref):
    j = pl.program_id(1)
    blk = q_ref.shape[1]
    q = q_ref[0]
    kk = jnp.concatenate([kp_ref[0], kc_ref[0]], axis=0)
    vv = jnp.concatenate([vp_ref[0], vc_ref[0]], axis=0)
    qi = lax.broadcasted_iota(I32, (blk, 2 * blk), 0)
    si = lax.broadcasted_iota(I32, (blk, 2 * blk), 1)
    rel = qi + blk - si
    valid = (rel >= 0) & (rel < blk) & ((si >= blk) | (j > 0))
    for g in range(N_KV_HEADS):
        kg = kk[:, g * HEAD_DIM:(g + 1) * HEAD_DIM]
        vg = vv[:, g * HEAD_DIM:(g + 1) * HEAD_DIM]
        for hh in range(Q_PER_KV):
            head = g * Q_PER_KV + hh
            qh = q[:, head * HEAD_DIM:(head + 1) * HEAD_DIM]
            s = lax.dot_general(qh, kg, (((1,), (1,)), ((), ())), preferred_element_type=F32)
            s = jnp.where(valid, s, -1e30)
            sink = sink_ref[head]
            m = jnp.maximum(jnp.max(s, axis=1, keepdims=True), sink)
            p = jnp.exp(s - m)
            denom = jnp.sum(p, axis=1, keepdims=True) + jnp.exp(sink - m)
            o = jnp.dot(p.astype(BF16), vg, preferred_element_type=F32) / denom
            o_ref[0, :, head * HEAD_DIM:(head + 1) * HEAD_DIM] = o.astype(BF16)


def _swa(sinks, q, k, v):
    b, seq, _ = q.shape
    nb = seq // ATTN_BLOCK
    prev = lambda bi, j: (bi, jnp.maximum(j - 1, 0), 0)
    cur = lambda bi, j: (bi, j, 0)
    return pl.pallas_call(
        _swa_kernel,
        grid=(b, nb),
        in_specs=[
            pl.BlockSpec(memory_space=pltpu.SMEM),
            pl.BlockSpec((1, ATTN_BLOCK, ATTN_WIDTH), cur),
            pl.BlockSpec((1, ATTN_BLOCK, KV_WIDTH), prev),
            pl.BlockSpec((1, ATTN_BLOCK, KV_WIDTH), cur),
            pl.BlockSpec((1, ATTN_BLOCK, KV_WIDTH), prev),
            pl.BlockSpec((1, ATTN_BLOCK, KV_WIDTH), cur),
        ],
        out_specs=pl.BlockSpec((1, ATTN_BLOCK, ATTN_WIDTH), cur),
        out_shape=jax.ShapeDtypeStruct((b, seq, ATTN_WIDTH), BF16),
        compiler_params=_params(("arbitrary", "arbitrary")),
        name="swa",
    )(sinks, q, k, k, v, v)


def _s5_prep_kernel(lr_ref, li_ref, ldt_ref, br_ref, bi_ref, ar_ref, ai_ref, bbr_ref, bbi_ref):
    lr = jnp.minimum(lr_ref[...], -1e-4)
    li = li_ref[...]
    dt = jnp.exp(ldt_ref[...])
    mag = jnp.exp(lr * dt)
    ar = mag * jnp.cos(li * dt)
    ai = mag * jnp.sin(li * dt)
    den = lr * lr + li * li
    zr = ((ar - 1.0) * lr + ai * li) / den
    zi = (ai * lr - (ar - 1.0) * li) / den
    ar_ref[...] = ar
    ai_ref[...] = ai
    br = br_ref[...]
    bi = bi_ref[...]
    bbr_ref[...] = zr[:, None, :] * br - zi[:, None, :] * bi
    bbi_ref[...] = zr[:, None, :] * bi + zi[:, None, :] * br


def _s5_prep(lam_re, lam_im, log_dt, b_re_t, b_im_t):
    g, p = lam_re.shape
    h = b_re_t.shape[1]
    return pl.pallas_call(
        _s5_prep_kernel,
        out_shape=[
            jax.ShapeDtypeStruct((g, p), F32),
            jax.ShapeDtypeStruct((g, p), F32),
            jax.ShapeDtypeStruct((g, h, p), F32),
            jax.ShapeDtypeStruct((g, h, p), F32),
        ],
        name="s5_prep",
    )(lam_re, lam_im, log_dt.reshape(g, 1), b_re_t, b_im_t)


def _s5_kernel(u_ref, wb_ref, ar_ref, ai_ref, wc_ref, d_ref, wg_ref, bg_ref, o_ref,
               st_ref, lhs_ref, buf_ref, y_ref):
    nb, t_len, width = u_ref.shape
    half = buf_ref.shape[1] // 2
    qw = width // S5_QUARTERS

    @pl.when(pl.program_id(0) == 0)
    def _():
        st_ref[...] = jnp.zeros_like(st_ref)
        lhs_ref[...] = jnp.zeros_like(lhs_ref)

    for bi in range(nb):
        for j in range(S5_QUARTERS):
            lhs_ref[j, pl.ds(bi * S5_QUARTERS + j, t_len, stride=SUBLANES), :] = u_ref[bi, :, j * qw:(j + 1) * qw]
    lhs = jnp.concatenate([lhs_ref[c] for c in range(S5_QUARTERS)], axis=1).astype(BF16)
    buf_ref[...] = jnp.dot(lhs, wb_ref[...], preferred_element_type=F32)

    ar = ar_ref[...]
    ai = ai_ref[...]

    def step(t, carry):
        sr, si = carry
        r0 = pl.multiple_of(t * SUBLANES, SUBLANES)
        nsr = ar * sr - ai * si + buf_ref[pl.ds(r0, SUBLANES), 0:half]
        nsi = ar * si + ai * sr + buf_ref[pl.ds(r0, SUBLANES), half:2 * half]
        buf_ref[pl.ds(r0, SUBLANES), 0:half] = nsr
        buf_ref[pl.ds(r0, SUBLANES), half:2 * half] = nsi
        return nsr, nsi

    sr, si = lax.fori_loop(0, t_len, step, (st_ref[:, 0:half], st_ref[:, half:2 * half]), unroll=8)
    st_ref[:, 0:half] = sr
    st_ref[:, half:2 * half] = si

    yall = jnp.dot(buf_ref[...].astype(BF16), wc_ref[...], preferred_element_type=F32)
    for c in range(S5_QUARTERS):
        y_ref[c] = yall[:, c * qw:(c + 1) * qw]
    for bi in range(nb):
        y = jnp.concatenate([y_ref[j, pl.ds(bi * S5_QUARTERS + j, t_len, stride=SUBLANES), :]
                             for j in range(S5_QUARTERS)], axis=1)
        g = jax.nn.gelu(y + d_ref[...] * u_ref[bi])
        z = jnp.dot(g.astype(BF16), wg_ref[...], preferred_element_type=F32) + bg_ref[...]
        o_ref[bi] = (g * jax.nn.sigmoid(z)).astype(BF16)


def _s5(u, wb, ar8, ai8, wc, d_skip, w_glu_bf, b_glu, t_len=256):
    b, seq, width = u.shape
    rows = b * S5_QUARTERS
    assert rows == SUBLANES and width // S5_QUARTERS == LANES
    sw = wb.shape[1]
    full = lambda i: (0, 0)
    return pl.pallas_call(
        _s5_kernel,
        grid=(seq // t_len,),
        in_specs=[
            pl.BlockSpec((b, t_len, width), lambda i: (0, i, 0)),
            pl.BlockSpec(wb.shape, full),
            pl.BlockSpec(ar8.shape, full),
            pl.BlockSpec(ai8.shape, full),
            pl.BlockSpec(wc.shape, full),
            pl.BlockSpec((1, width), full),
            pl.BlockSpec((width, width), full),
            pl.BlockSpec((1, width), full),
        ],
        out_specs=pl.BlockSpec((b, t_len, width), lambda i: (0, i, 0)),
        out_shape=jax.ShapeDtypeStruct((b, seq, width), BF16),
        scratch_shapes=[
            pltpu.VMEM((rows, sw), F32),
            pltpu.VMEM((S5_QUARTERS, rows * t_len, width // S5_QUARTERS), F32),
            pltpu.VMEM((rows * t_len, sw), F32),
            pltpu.VMEM((S5_QUARTERS, rows * t_len, width // S5_QUARTERS), F32),
        ],
        compiler_params=_params(("arbitrary",)),
        name="s5_scan",
    )(u, wb, ar8, ai8, wc, d_skip, w_glu_bf, b_glu)


def _outproj_kernel(a_ref, s_ref, x_ref, wa_ref, ws_ref, b_ref, g_ref, be_ref, o_ref):
    mix = (jnp.dot(a_ref[...], wa_ref[...], preferred_element_type=F32)
           + jnp.dot(s_ref[...], ws_ref[...], preferred_element_type=F32) + b_ref[...])
    o_ref[...] = _layer_norm(DEEPNORM_ALPHA * x_ref[...] + mix, g_ref[...], be_ref[...])


def _outproj(attn, ssm, x2d, wa, ws, b, g, be, tm=512):
    n, d = x2d.shape
    full = lambda i: (0, 0)
    row = lambda i: (i, 0)
    return pl.pallas_call(
        _outproj_kernel,
        grid=(n // tm,),
        in_specs=[
            pl.BlockSpec((tm, attn.shape[1]), row),
            pl.BlockSpec((tm, ssm.shape[1]), row),
            pl.BlockSpec((tm, d), row),
            pl.BlockSpec(wa.shape, full),
            pl.BlockSpec(ws.shape, full),
            pl.BlockSpec((1, d), full),
            pl.BlockSpec((1, d), full),
            pl.BlockSpec((1, d), full),
        ],
        out_specs=pl.BlockSpec((tm, d), row),
        out_shape=jax.ShapeDtypeStruct((n, d), F32),
        compiler_params=_params(("arbitrary",)),
        name="outproj_ln1",
    )(attn, ssm, x2d, wa, ws, b, g, be)


def _memkv_kernel(m_ref, w_ref, k_ref, v_ref):
    d = k_ref.shape[1]
    kv = jnp.dot(m_ref[...].astype(BF16), w_ref[...], preferred_element_type=F32)
    k_ref[...] = kv[:, :d].astype(BF16)
    v_ref[...] = kv[:, d:].astype(BF16)


def _memkv(mem2d, w_bf):
    n, d = mem2d.shape
    return pl.pallas_call(
        _memkv_kernel,
        out_shape=[jax.ShapeDtypeStruct((n, d), BF16), jax.ShapeDtypeStruct((n, d), BF16)],
        compiler_params=_params(None),
        name="mem_kv",
    )(mem2d, w_bf)


def _xattn_kernel(x_ref, wq_ref, k_ref, v_ref, wo_ref, g_ref, be_ref, wr_ref, br_ref,
                  x2_ref, idx_ref, gate_ref):
    tm, d = x_ref.shape
    hd = d // X_HEADS
    x1 = x_ref[...]
    q = jnp.dot(x1.astype(BF16), wq_ref[...], preferred_element_type=F32) * (hd ** -0.5)
    q = q.astype(BF16)
    k = k_ref[0]
    v = v_ref[0]
    outs = []
    for h in range(X_HEADS):
        sl = slice(h * hd, (h + 1) * hd)
        s = lax.dot_general(q[:, sl], k[:, sl], (((1,), (1,)), ((), ())), preferred_element_type=F32)
        m = jnp.max(s, axis=1, keepdims=True)
        p = jnp.exp(s - m)
        denom = jnp.sum(p, axis=1, keepdims=True)
        outs.append((jnp.dot(p.astype(BF16), v[:, sl], preferred_element_type=F32) / denom).astype(BF16))
    o = jnp.concatenate(outs, axis=1)
    xa = jnp.dot(o, wo_ref[...], preferred_element_type=F32)
    x2 = _layer_norm(DEEPNORM_ALPHA * x1 + xa, g_ref[...], be_ref[...])
    x2_ref[...] = x2

    x_hi = x2.astype(BF16)
    x_lo = (x2 - x_hi.astype(F32)).astype(BF16)
    logits = (jnp.dot(jnp.concatenate([x_hi, x_lo, x_hi], axis=1), wr_ref[...], preferred_element_type=F32)
              + br_ref[...])
    ne = logits.shape[1]
    lane = lax.broadcasted_iota(I32, (tm, ne), 1).astype(F32)
    lane_out = lax.broadcasted_iota(I32, (tm, TOP_K), 1)
    vals, idxs = [], []
    l = logits
    for _ in range(TOP_K):
        m = jnp.max(l, axis=1, keepdims=True)
        ix = jnp.min(jnp.where(l == m, lane, float(ne)), axis=1, keepdims=True)
        vals.append(m)
        idxs.append(ix)
        l = jnp.where(lane == ix, -jnp.inf, l)
    es = [jnp.exp(vk - vals[0]) for vk in vals]
    tot = es[0] + es[1] + es[2] + es[3]
    idx_out = jnp.zeros((tm, TOP_K), F32)
    gate_out = jnp.zeros((tm, TOP_K), F32)
    for kk in range(TOP_K):
        idx_out = jnp.where(lane_out == kk, idxs[kk], idx_out)
        gate_out = jnp.where(lane_out == kk, es[kk] / tot, gate_out)
    idx_ref[...] = idx_out.astype(I32)
    gate_ref[...] = gate_out


def _xattn(x1, wq, kmem, vmem, wo, g, be, w_router, b_router, seq, tm=512):
    n, d = x1.shape
    nm = kmem.shape[1]
    ne = w_router.shape[1]
    full = lambda i: (0, 0)
    row = lambda i: (i, 0)
    mem_map = lambda i: ((i * tm) // seq, 0, 0)
    return pl.pallas_call(
        _xattn_kernel,
        grid=(n // tm,),
        in_specs=[
            pl.BlockSpec((tm, d), row),
            pl.BlockSpec((d, d), full),
            pl.BlockSpec((1, nm, d), mem_map),
            pl.BlockSpec((1, nm, d), mem_map),
            pl.BlockSpec((d, d), full),
            pl.BlockSpec((1, d), full),
            pl.BlockSpec((1, d), full),
            pl.BlockSpec(w_router.shape, full),
            pl.BlockSpec((1, ne), full),
        ],
        out_specs=[
            pl.BlockSpec((tm, d), row),
            pl.BlockSpec((tm, TOP_K), row),
            pl.BlockSpec((tm, TOP_K), row),
        ],
        out_shape=[
            jax.ShapeDtypeStruct((n, d), F32),
            jax.ShapeDtypeStruct((n, TOP_K), I32),
            jax.ShapeDtypeStruct((n, TOP_K), F32),
        ],
        compiler_params=_params(("arbitrary",)),
        name="xattn_ln2_router",
    )(x1, wq, kmem, vmem, wo, g, be, w_router, b_router)


def _route_kernel(idx_ref, dest_ref, meta_ref, cnt_ref, carry_ref, pstart_ref):
    ph = pl.program_id(0)
    i = pl.program_id(1)
    tm = idx_ref.shape[0]
    ne = carry_ref.shape[1]
    idx = idx_ref[...]
    lane = lax.broadcasted_iota(I32, (tm, ne), 1)
    r0 = pl.multiple_of(i * tm, tm)

    @pl.when((ph == 0) & (i == 0))
    def _():
        carry_ref[...] = jnp.zeros_like(carry_ref)

    @pl.when(ph == 0)
    def _():
        oh = jnp.zeros((tm, ne), F32)
        for k in range(TOP_K):
            oh = oh + (idx[:, k:k + 1] == lane).astype(F32)
        rr = lax.broadcasted_iota(I32, (tm, tm), 0)
        cc = lax.broadcasted_iota(I32, (tm, tm), 1)
        tri = (rr > cc).astype(BF16)
        pre = jnp.dot(tri, oh.astype(BF16), preferred_element_type=F32) + carry_ref[...]
        cnt_ref[pl.ds(r0, tm), :] = pre
        carry_ref[...] = carry_ref[...] + jnp.sum(oh, axis=0, keepdims=True)

    @pl.when((ph == 1) & (i == 0))
    def _():
        tot = jnp.broadcast_to(carry_ref[...], (SUBLANES, ne))
        padded = jnp.floor((tot + (MOE_BLOCK - 1)) * (1.0 / MOE_BLOCK)) * MOE_BLOCK
        er = lax.broadcasted_iota(I32, (ne, ne), 0)
        ec = lax.broadcasted_iota(I32, (ne, ne), 1)
        upper = (er < ec).astype(F32)
        ps = jnp.dot(padded, upper, preferred_element_type=F32, precision=lax.Precision.HIGHEST)
        pstart_ref[...] = ps[0:1, :]
        row = lax.broadcasted_iota(I32, (SUBLANES, ne), 0)
        meta_ref[...] = jnp.where(row == 0, tot, jnp.where(row == 1, ps, 0.0)).astype(I32)

    @pl.when(ph == 1)
    def _():
        pos = cnt_ref[pl.ds(r0, tm), :] + pstart_ref[...]
        lane_out = lax.broadcasted_iota(I32, (tm, TOP_K), 1)
        out = jnp.zeros((tm, TOP_K), F32)
        for k in range(TOP_K):
            dk = jnp.sum(jnp.where(idx[:, k:k + 1] == lane, pos, 0.0), axis=1, keepdims=True)
            out = jnp.where(lane_out == k, dk, out)
        dest_ref[...] = out.astype(I32)


def _route(idx, tm=512):
    n = idx.shape[0]
    return pl.pallas_call(
        _route_kernel,
        grid=(2, n // tm),
        in_specs=[pl.BlockSpec((tm, TOP_K), lambda p, i: (i, 0))],
        out_specs=[
            pl.BlockSpec((tm, TOP_K), lambda p, i: (i * p, 0)),
            pl.BlockSpec((SUBLANES, N_EXPERTS), lambda p, i: (0, 0)),
        ],
        out_shape=[
            jax.ShapeDtypeStruct((n, TOP_K), I32),
            jax.ShapeDtypeStruct((SUBLANES, N_EXPERTS), I32),
        ],
        scratch_shapes=[
            pltpu.VMEM((n, N_EXPERTS), F32),
            pltpu.VMEM((1, N_EXPERTS), F32),
            pltpu.VMEM((1, N_EXPERTS), F32),
        ],
        compiler_params=_params(("arbitrary", "arbitrary")),
        name="route",
    )(idx)


def _invert_kernel(dest_ref, inv_ref):
    ph = pl.program_id(0)
    j = pl.program_id(1)
    n_asg = dest_ref.shape[0]
    n_tok = n_asg // TOP_K
    fill_rows = inv_ref.shape[0] // INVERT_STEPS
    tok_rows = n_tok // INVERT_STEPS

    @pl.when(ph == 0)
    def _():
        def fill(q, c):
            row = j * fill_rows + q
            inv_ref[row] = n_asg + (row & (MOE_BLOCK - 1))
            return c

        lax.fori_loop(0, fill_rows, fill, 0, unroll=8)

    @pl.when(ph == 1)
    def _():
        def place(q, c):
            t = j * tok_rows + q
            for k in range(TOP_K):
                inv_ref[dest_ref[t * TOP_K + k] + MOE_BLOCK] = k * n_tok + t
            return c

        lax.fori_loop(0, tok_rows, place, 0, unroll=8)


def _invert(dest_flat, n_rows):
    total = n_rows + 2 * MOE_BLOCK
    assert total % INVERT_STEPS == 0 and (dest_flat.shape[0] // TOP_K) % INVERT_STEPS == 0
    return pl.pallas_call(
        _invert_kernel,
        grid=(2, INVERT_STEPS),
        in_specs=[pl.BlockSpec(memory_space=pltpu.SMEM)],
        out_specs=pl.BlockSpec(memory_space=pltpu.SMEM),
        out_shape=jax.ShapeDtypeStruct((total,), I32),
        compiler_params=_params(("arbitrary", "arbitrary")),
        name="invert",
    )(dest_flat)


def _expert_kernel(be_ref, nu_ref, inv_ref, x_hbm, w1_ref, w2_ref, b1g_ref, b1l_ref, b2_ref, perm_ref, y_hbm,
                   wg_ref, wl_ref, w2b_ref, xbuf0, xbuf1, ybuf0, ybuf1, gsem, ssem):
    i = pl.program_id(0)
    nu = nu_ref[0]
    e = be_ref[i]
    prev = be_ref[jnp.maximum(i - 1, 0)]
    used = i < nu
    half = perm_ref.shape[0] // 2
    n_tok = x_hbm.shape[0]
    tok_mask = n_tok - 1
    assert n_tok & tok_mask == 0, "token count must be a power of two (assignment id -> token by masking)"
    xbufs = (xbuf0, xbuf1)
    ybufs = (ybuf0, ybuf1)

    def gather_start(blk, xbuf):
        base = (blk + 1) * MOE_BLOCK
        for r in range(MOE_BLOCK):
            tok = inv_ref[base + r] & tok_mask
            pltpu.make_async_copy(x_hbm.at[pl.ds(tok, 1)], xbuf.at[pl.ds(r, 1)], gsem).start()

    def scatter_start(blk, ybuf):
        base = (blk + 1) * MOE_BLOCK
        for r in range(MOE_BLOCK):
            pltpu.make_async_copy(ybuf.at[pl.ds(r, 1)], y_hbm.at[pl.ds(inv_ref[base + r], 1)], ssem).start()

    def gather_wait():
        for r in range(MOE_BLOCK):
            pltpu.make_async_copy(x_hbm.at[pl.ds(0, 1)], xbuf0.at[pl.ds(0, 1)], gsem).wait()

    def scatter_wait():
        for r in range(MOE_BLOCK):
            pltpu.make_async_copy(ybuf0.at[pl.ds(0, 1)], y_hbm.at[pl.ds(0, 1)], ssem).wait()

    @pl.when(i == 0)
    def _():
        gather_start(0, xbuf0)
        ybuf1[...] = jnp.zeros_like(ybuf1)

    @pl.when((i >= 1) & (i <= nu))
    def _():
        scatter_wait()

    @pl.when(i <= nu)
    def _():
        gather_wait()

    @pl.when(used & ((i == 0) | (e != prev)))
    def _():
        perm = perm_ref[...]
        for c in range(w1_ref.shape[2] // (2 * half)):
            w1c = w1_ref[0, :, c * 2 * half:(c + 1) * 2 * half].astype(BF16)
            sep = jnp.dot(w1c, perm, preferred_element_type=F32).astype(BF16)
            wg_ref[:, c * half:(c + 1) * half] = sep[:, :half]
            wl_ref[:, c * half:(c + 1) * half] = sep[:, half:]
        w2b_ref[...] = w2_ref[0].astype(BF16)

    parity = lax.rem(i, 2)
    for s in range(2):
        @pl.when(used & (parity == s))
        def _(s=s):
            gather_start(i + 1, xbufs[1 - s])
            scatter_start(i - 1, ybufs[1 - s])
            xb = xbufs[s][...].astype(BF16)
            hg = jnp.dot(xb, wg_ref[...], preferred_element_type=F32) + b1g_ref[0]
            hl = jnp.dot(xb, wl_ref[...], preferred_element_type=F32) + b1l_ref[0]
            gt = jnp.minimum(hg, SWIGLU_LIMIT)
            lin = jnp.clip(hl, -SWIGLU_LIMIT, SWIGLU_LIMIT)
            act = gt * jax.nn.sigmoid(SWIGLU_ALPHA * gt) * (lin + 1.0)
            ybufs[s][...] = jnp.dot(act.astype(BF16), w2b_ref[...], preferred_element_type=F32) + b2_ref[0]

        @pl.when((i == nu) & (parity == s))
        def _(s=s):
            scatter_start(i - 1, ybufs[1 - s])
            scatter_wait()


def _experts(block_e, n_used, inv, x2, w1, w2, b1g, b1l, b2, perm):
    n_tok, d = x2.shape
    nblk = inv.shape[0] // MOE_BLOCK - 2
    ne, _, d2 = w1.shape
    de = w2.shape[1]
    wmap = lambda i, be, nu, iv: (be[i], 0, 0)
    return pl.pallas_call(
        _expert_kernel,
        grid_spec=pltpu.PrefetchScalarGridSpec(
            num_scalar_prefetch=3,
            grid=(nblk,),
            in_specs=[
                pl.BlockSpec(memory_space=pl.ANY),
                pl.BlockSpec((1, d, d2), wmap),
                pl.BlockSpec((1, de, d), wmap),
                pl.BlockSpec((1, 1, de), wmap),
                pl.BlockSpec((1, 1, de), wmap),
                pl.BlockSpec((1, 1, d), wmap),
                pl.BlockSpec(perm.shape, lambda i, be, nu, iv: (0, 0)),
            ],
            out_specs=pl.BlockSpec(memory_space=pl.ANY),
            scratch_shapes=[
                pltpu.VMEM((d, de), BF16),
                pltpu.VMEM((d, de), BF16),
                pltpu.VMEM((de, d), BF16),
                pltpu.VMEM((MOE_BLOCK, d), F32),
                pltpu.VMEM((MOE_BLOCK, d), F32),
                pltpu.VMEM((MOE_BLOCK, d), F32),
                pltpu.VMEM((MOE_BLOCK, d), F32),
                pltpu.SemaphoreType.DMA,
                pltpu.SemaphoreType.DMA,
            ],
        ),
        out_shape=jax.ShapeDtypeStruct((TOP_K * n_tok + MOE_BLOCK, d), F32),
        compiler_params=_params(("arbitrary",)),
        name="experts",
    )(block_e, n_used, inv, x2, w1, w2, b1g, b1l, b2, perm)


def _combine_kernel(y0_ref, y1_ref, y2_ref, y3_ref, x_ref, gate_ref, g_ref, be_ref, o_ref):
    gate = gate_ref[...]
    ff = gate[:, 0:1] * y0_ref[...]
    for k, y_ref in enumerate((y1_ref, y2_ref, y3_ref), start=1):
        ff = ff + gate[:, k:k + 1] * y_ref[...]
    o_ref[...] = _layer_norm(DEEPNORM_ALPHA * x_ref[...] + ff, g_ref[...], be_ref[...])


def _combine(yassign, x2, gate, g, be, tm=512):
    n, d = x2.shape
    nt = n // tm
    full = lambda i: (0, 0)
    row = lambda i: (i, 0)
    yspecs = [pl.BlockSpec((tm, d), functools.partial(lambda i, k: (k * nt + i, 0), k=k)) for k in range(TOP_K)]
    return pl.pallas_call(
        _combine_kernel,
        grid=(nt,),
        in_specs=yspecs + [
            pl.BlockSpec((tm, d), row),
            pl.BlockSpec((tm, TOP_K), row),
            pl.BlockSpec((1, d), full),
            pl.BlockSpec((1, d), full),
        ],
        out_specs=pl.BlockSpec((tm, d), row),
        out_shape=jax.ShapeDtypeStruct((n, d), F32),
        compiler_params=_params(("arbitrary",)),
        name="combine_ln3",
    )(yassign, yassign, yassign, yassign, x2, gate, g, be)


def _rope_tables(seq):
    half = HEAD_DIM // 2
    inv_freq = ROPE_THETA ** (-jnp.arange(half, dtype=F32) * (2.0 / HEAD_DIM))
    ang = jnp.arange(seq, dtype=F32)[:, None] * inv_freq[None, :]
    cos, sin = jnp.cos(ang), jnp.sin(ang)
    reps = LANES // HEAD_DIM
    cos_t = jnp.tile(jnp.concatenate([cos, cos], axis=1), (1, reps))
    sin_t = jnp.tile(jnp.concatenate([-sin, sin], axis=1), (1, reps))
    return cos_t, sin_t


def _s5_matrices(ar, ai, bbr, bbi, c_re_t, c_im_t, batch):
    gq = S5_GROUPS // S5_QUARTERS
    eye = jnp.eye(gq, dtype=F32)

    def in_mat(bb):
        bb4 = bb.reshape(S5_QUARTERS, gq, S5_GROUP_CH, S5_STATE)
        return jnp.einsum('jghp,gk->jghkp', bb4, eye).reshape(S5_WIDTH, gq * S5_STATE)

    def out_mat(cc):
        cc4 = cc.reshape(S5_QUARTERS, gq, S5_GROUP_CH, S5_STATE)
        return jnp.einsum('jghp,gk->kpjgh', cc4, eye).reshape(gq * S5_STATE, S5_WIDTH)

    wb = jnp.concatenate([in_mat(bbr), in_mat(bbi)], axis=1).astype(BF16)
    wc = jnp.concatenate([out_mat(c_re_t), -out_mat(c_im_t)], axis=0).astype(BF16)
    ar8 = jnp.tile(ar.reshape(S5_QUARTERS, gq * S5_STATE), (batch, 1))
    ai8 = jnp.tile(ai.reshape(S5_QUARTERS, gq * S5_STATE), (batch, 1))
    return wb, wc, ar8, ai8


def _deinterleave_perm():
    n = 2 * LANES
    src = jnp.arange(n)
    dst = jnp.where(src % 2 == 0, src // 2, LANES + src // 2)
    return (dst[:, None] == jnp.arange(n)[None, :]).astype(BF16)


def kernel(x, mem, w_in, b_in, attn_sinks, s5_lambda_re, s5_lambda_im, s5_log_dt, s5_b_re, s5_b_im, s5_c_re, s5_c_im, s5_d, s5_w_glu, s5_b_glu, w_out, b_out, ln1_g, ln1_b, w_xq, w_xkv, w_xo, ln2_g, ln2_b, w_router, b_router, w_e1, b_e1, w_e2, b_e2, ln3_g, ln3_b):
    B, L, D = x.shape
    N = B * L
    cos_t, sin_t = _rope_tables(L)
    perm = _deinterleave_perm()
    row2 = lambda a: a.reshape(1, -1)
    h = x.reshape(N, D)
    for l in range(DEPTH):
        q, k, v, u = _inproj(h, w_in[l].astype(BF16), row2(b_in[l]), cos_t, sin_t, L)
        attn = _swa(attn_sinks[l], q.reshape(B, L, -1), k.reshape(B, L, -1), v.reshape(B, L, -1))
        ar, ai, bbr, bbi = _s5_prep(s5_lambda_re[l], s5_lambda_im[l], s5_log_dt[l],
                                    jnp.swapaxes(s5_b_re[l], 1, 2), jnp.swapaxes(s5_b_im[l], 1, 2))
        wb, wc, ar8, ai8 = _s5_matrices(ar, ai, bbr, bbi, s5_c_re[l], s5_c_im[l], B)
        ssm = _s5(u.reshape(B, L, -1), wb, ar8, ai8, wc, row2(s5_d[l]),
                  s5_w_glu[l].astype(BF16), row2(s5_b_glu[l]))
        wo = w_out[l].astype(BF16)
        x1 = _outproj(attn.reshape(N, -1), ssm.reshape(N, -1), h, wo[:ATTN_WIDTH], wo[ATTN_WIDTH:],
                      row2(b_out[l]), row2(ln1_g[l]), row2(ln1_b[l]))
        kmem, vmem = _memkv(mem.reshape(-1, D), w_xkv[l].astype(BF16))
        wr_hi = w_router[l].astype(BF16)
        wr_lo = (w_router[l] - wr_hi.astype(F32)).astype(BF16)
        x2, idx, gate = _xattn(x1, w_xq[l].astype(BF16), kmem.reshape(B, -1, D), vmem.reshape(B, -1, D),
                               w_xo[l].astype(BF16), row2(ln2_g[l]), row2(ln2_b[l]),
                               jnp.concatenate([wr_hi, wr_hi, wr_lo], axis=0), row2(b_router[l]), L)
        dest, meta = _route(idx)
        padded = (meta[0] + (MOE_BLOCK - 1)) // MOE_BLOCK * MOE_BLOCK
        pends = meta[1] + padded
        n_blocks = N * TOP_K // MOE_BLOCK + N_EXPERTS
        block_start = jnp.arange(n_blocks, dtype=I32) * MOE_BLOCK
        block_e = jnp.minimum(jnp.sum((pends[None, :] <= block_start[:, None]).astype(I32), axis=1), N_EXPERTS - 1)
        n_used = (pends[-1:] // MOE_BLOCK).astype(I32)
        inv = _invert(dest.reshape(N * TOP_K), n_blocks * MOE_BLOCK)
        de = w_e2.shape[2]
        yassign = _experts(block_e, n_used, inv, x2, w_e1[l], w_e2[l],
                           b_e1[l][:, 0::2].reshape(N_EXPERTS, 1, de), b_e1[l][:, 1::2].reshape(N_EXPERTS, 1, de),
                           b_e2[l].reshape(N_EXPERTS, 1, D), perm)
        h = _combine(yassign, x2, gate, row2(ln3_g[l]), row2(ln3_b[l]))
    return h.reshape(B, L, D)
```
